```python
import math
import jax, jax.numpy as jnp
from jax import lax
import numpy as np

D_MODEL = 2048
BATCH = 4
SEQ = 2048
DEPTH = 4
DEC_BATCH = 128
DEC_SEQ = 1
PAST_LEN = 16384
PAGE_SIZE = 128

N_MEM = 256
XA_HEADS = 4
XA_HD = 128
XA_W = XA_HEADS * XA_HD
RET_DK = 64
RET_DV = 128
RET_HEADS = D_MODEL // 4 // RET_DV
RET_QK = RET_HEADS * RET_DK
RET_W = RET_HEADS * RET_DV
HG_DK = 128
HG_DV = 128
HG_HEADS = D_MODEL // 4 // HG_DV
HG_QK = HG_HEADS * HG_DK
HG_W = HG_HEADS * HG_DV
RW_HD = 64
RW_HEADS = D_MODEL // 2 // RW_HD
RW_W = RW_HEADS * RW_HD
RW_DECAY_LORA = 64
RW_A_LORA = 64
RW_GATE_LORA = 128
RW_GN_EPS = 64e-5
MAIN_SIZES = (RET_QK, RET_QK, RET_W, RET_W, HG_QK, HG_QK, HG_W, HG_W)
RW_SIZES = (RW_W, RW_DECAY_LORA, RW_W, RW_W, RW_A_LORA, RW_GATE_LORA)
N_MAIN_COLS = sum(MAIN_SIZES)
N_RW_COLS = sum(RW_SIZES)
N_IN_COLS = N_MAIN_COLS + N_RW_COLS
CHUNK = 64
ROPE_BASE = 10000.0
N_EXP = 32
TOP_K = 4
D_FF = D_MODEL
SWIGLU_LIMIT = 7.0
SWIGLU_ALPHA = 1.702
DN_ALPHA = (2.0 * DEPTH) ** 0.25
DN_BETA = (8.0 * DEPTH) ** -0.25
LN_EPS = 1e-5

kernel_name = 'hymba_ret_hgrn2_rwkv7_moe_step'


def _rand(key, shape, std, mean=0.0):
    a = std * math.sqrt(3.0)
    return jax.random.uniform(key, shape, jnp.float32, mean - a, mean + a)


def _split(a, sizes):
    return jnp.split(a, np.cumsum(sizes)[:-1].tolist(), axis=-1)


def layer_norm(x, g, b):
    xf = x.astype(jnp.float32)
    mu = jnp.mean(xf, -1, keepdims=True)
    var = jnp.mean(jnp.square(xf - mu), -1, keepdims=True)
    y = (xf - mu) * lax.rsqrt(var + LN_EPS)
    return (y * g.astype(jnp.float32) + b.astype(jnp.float32)).astype(x.dtype)


def head_rms_norm(x, g, eps=1e-6):
    x = x.astype(jnp.float32)
    y = x * lax.rsqrt(jnp.mean(x * x, -1, keepdims=True) + eps)
    return y.reshape(*x.shape[:-2], -1) * g.astype(jnp.float32)


def head_group_norm(x, g, b, eps):
    x = x.astype(jnp.float32)
    mu = jnp.mean(x, -1, keepdims=True)
    var = jnp.mean(jnp.square(x - mu), -1, keepdims=True)
    y = (x - mu) * lax.rsqrt(var + eps)
    return y.reshape(*x.shape[:-2], -1) * g.astype(jnp.float32) + b.astype(jnp.float32)


def rotary(x, pos):
    x = x.astype(jnp.float32)
    half = x.shape[-1] // 2
    inv_freq = jnp.power(ROPE_BASE, -jnp.arange(half, dtype=jnp.float32) / half)
    ang = pos.astype(jnp.float32)[:, None] * inv_freq[None, :]
    cos = jnp.cos(ang)[None, :, None, :]
    sin = jnp.sin(ang)[None, :, None, :]
    x1, x2 = x[..., :half], x[..., half:]
    return jnp.concatenate([x1 * cos - x2 * sin, x1 * sin + x2 * cos], axis=-1)


def _chunk_len(T):
    return CHUNK if T % CHUNK == 0 else T


def _to_chunks(a, c):
    B, T, H, d = a.shape
    return a.astype(jnp.float32).reshape(B, T // c, c, H, d).transpose(1, 0, 3, 2, 4)


def _from_chunks(a):
    n, B, H, c, d = a.shape
    return a.transpose(1, 0, 3, 2, 4).reshape(B, n * c, H, d)


def retention_chunkwise(q, k, v, S0):
    H = q.shape[2]
    c = _chunk_len(q.shape[1])
    log_gamma = jnp.log1p(-jnp.exp2(-5.0 - jnp.arange(H, dtype=jnp.float32)))
    t = jnp.arange(c, dtype=jnp.float32)
    lag = t[:, None] - t[None, :]
    causal = lag >= 0
    dmat = jnp.where(causal, jnp.exp(jnp.where(causal, lag, 0.0)[None] * log_gamma[:, None, None]), 0.0)
    q_decay = jnp.exp((t + 1.0)[None, :] * log_gamma[:, None])[None, :, :, None]
    k_decay = jnp.exp((c - 1.0 - t)[None, :] * log_gamma[:, None])[None, :, :, None]
    chunk_decay = jnp.exp(c * log_gamma)[None, :, None, None]

    def step(S, blk):
        qb, kb, vb = blk
        scores = jnp.einsum('bhtd,bhsd->bhts', qb, kb) * dmat
        o = jnp.einsum('bhts,bhse->bhte', scores, vb) + jnp.einsum('bhtd,bhde->bhte', qb, S) * q_decay
        S = S * chunk_decay + jnp.einsum('bhsd,bhse->bhde', kb * k_decay, vb)
        return S, o

    S, o = lax.scan(step, S0.astype(jnp.float32), (_to_chunks(q, c), _to_chunks(k, c), _to_chunks(v, c)))
    return _from_chunks(o), S


def hgrn2_chunkwise(q, k, v, log_f, S0):
    c = _chunk_len(q.shape[1])
    causal = (jnp.arange(c)[:, None] >= jnp.arange(c)[None, :])[None, None, :, :, None]

    def step(S, blk):
        qb, kb, vb, gb = blk
        b = jnp.cumsum(gb, axis=2)
        rel = jnp.exp(jnp.where(causal, b[:, :, :, None, :] - b[:, :, None, :, :], -jnp.inf))
        A = jnp.einsum('bhtd,bhsd,bhtsd->bhts', qb, kb, rel)
        o = jnp.einsum('bhts,bhse->bhte', A, vb) + jnp.einsum('bhtd,bhde->bhte', qb * jnp.exp(b), S)
        b_end = b[:, :, -1:, :]
        S = S * jnp.exp(b_end)[:, :, 0, :, None] + jnp.einsum('bhsd,bhse->bhde', kb * jnp.exp(b_end - b), vb)
        return S, o

    S, o = lax.scan(step, S0.astype(jnp.float32),
                    (_to_chunks(q, c), _to_chunks(k, c), _to_chunks(v, c), _to_chunks(log_f, c)))
    return _from_chunks(o), S


def rwkv7_recurrent(r, w, k, v, kk, a, S0):
    def step(S, inp):
        rt, wt, kt, vt, kkt, at = inp
        sa = jnp.einsum('bhij,bhj->bhi', S, -kkt)
        S = S * wt[:, :, None, :] + sa[..., None] * (kkt * at)[:, :, None, :] + vt[..., None] * kt[:, :, None, :]
        return S, jnp.einsum('bhij,bhj->bhi', S, rt)

    xs = tuple(jnp.moveaxis(t.astype(jnp.float32), 1, 0) for t in (r, w, k, v, kk, a))
    S, y = lax.scan(step, S0.astype(jnp.float32), xs)
    return jnp.moveaxis(y, 0, 1), S


def token_mixer(h, prev_h, pos, s_ret, s_hg, s_rw, lb, lw):
    f32 = jnp.float32
    B, T, _ = h.shape
    heads = lambda a, n: a.reshape(B, T, n, -1)
    proj = h @ lw['w_in']
    main, rw = proj[..., :N_MAIN_COLS], proj[..., N_MAIN_COLS:]
    rw_first_prev = (prev_h @ lw['w_in'][:, N_MAIN_COLS:])[:, None, :]
    rw_prev = jnp.concatenate([rw_first_prev, rw[:, :-1]], axis=1)
    rw = rw + lw['mu_shift'] * (rw_prev - rw)
    rq, rk, rv, rg, hq, hf, hi, hg = _split(main, MAIN_SIZES)
    xr, xw, xk, xv, xa, xg = [t.astype(f32) for t in _split(rw, RW_SIZES)]

    q = rotary(heads(rq, RET_HEADS), pos)
    k = rotary(heads(rk, RET_HEADS), pos) * RET_DK ** -0.5
    o, s_ret = retention_chunkwise(q, k, heads(rv, RET_HEADS), s_ret)
    out_a = (head_rms_norm(o, lw['ret_gn']) * jax.nn.silu(rg.astype(f32))).astype(h.dtype)

    hf32 = hf.astype(f32)
    log_f = jnp.logaddexp(jnp.log(lb), jnp.log1p(-lb) + jax.nn.log_sigmoid(hf32))
    k_in = (1.0 - lb) * jax.nn.sigmoid(-hf32)
    o, s_hg = hgrn2_chunkwise(heads(jax.nn.silu(hq.astype(f32)), HG_HEADS), heads(k_in, HG_HEADS),
                              heads(hi, HG_HEADS), heads(log_f, HG_HEADS), s_hg)
    out_b = (head_rms_norm(o, lw['hg_gn']) * jax.nn.silu(hg.astype(f32))).astype(h.dtype)

    w_log = -jax.nn.softplus(-(lw['rw_w0'] + jnp.tanh(xw) @ lw['rw_w2'])) - 0.5
    decay = jnp.exp(-jnp.exp(w_log))
    a = jax.nn.sigmoid(lw['rw_a0'] + xa @ lw['rw_a2'])
    g = jax.nn.sigmoid(xg) @ lw['rw_g2']
    kk = heads(xk * lw['rw_kk'], RW_HEADS)
    kk = kk / jnp.maximum(jnp.sqrt(jnp.sum(kk * kk, -1, keepdims=True)), 1e-12)
    kr = xk * (1.0 + (a - 1.0) * lw['rw_ka'])
    r_h, k_h, v_h = heads(xr, RW_HEADS), heads(kr, RW_HEADS), heads(xv, RW_HEADS)
    y, s_rw = rwkv7_recurrent(r_h, heads(decay, RW_HEADS), k_h, v_h, kk, heads(a, RW_HEADS), s_rw)
    y = head_group_norm(y, lw['rw_lnx_g'], lw['rw_lnx_b'], RW_GN_EPS)
    bonus = (jnp.sum(r_h * k_h * lw['rw_rk'].astype(f32), -1, keepdims=True) * v_h).reshape(B, T, RW_W)
    out_c = ((y + bonus) * g).astype(h.dtype)

    mixed = jnp.concatenate([out_a, out_b, out_c], axis=-1) @ lw['w_out']
    return mixed, h[:, -1], s_ret.astype(h.dtype), s_hg.astype(h.dtype), s_rw.astype(h.dtype)


def cross_attend(x, mem_k, mem_v, wq, wo):
    B, T, _ = x.shape
    q = (x @ wq).reshape(B, T, XA_HEADS, XA_HD)
    s = jnp.einsum('bthd,bmhd->bhtm', q, mem_k).astype(jnp.float32) * XA_HD ** -0.5
    p = jax.nn.softmax(s, axis=-1).astype(x.dtype)
    o = jnp.einsum('bhtm,bmhd->bthd', p, mem_v).reshape(B, T, XA_W)
    return o @ wo


def moe(x, wr, br, wgu, bgu, wd, bd):
    B, T, D = x.shape
    xt = x.reshape(B * T, D)
    logits = (xt @ wr + br).astype(jnp.float32)
    top_v, top_i = lax.top_k(logits, TOP_K)
    gates = jax.nn.softmax(top_v, axis=-1)
    comb = jnp.sum(jax.nn.one_hot(top_i, N_EXP, dtype=jnp.float32) * gates[..., None], axis=1).astype(x.dtype)

    def expert(acc, e):
        wgu_e, bgu_e, wd_e, bd_e, c_e = e
        gu = xt @ wgu_e + bgu_e
        gate = jnp.minimum(gu[:, :D_FF], SWIGLU_LIMIT)
        up = jnp.clip(gu[:, D_FF:], -SWIGLU_LIMIT, SWIGLU_LIMIT)
        hdn = (up + 1.0) * gate * jax.nn.sigmoid(SWIGLU_ALPHA * gate)
        return acc + c_e[:, None] * (hdn @ wd_e + bd_e), None

    acc, _ = lax.scan(expert, jnp.zeros_like(xt), (wgu, bgu, wd, bd, comb.T))
    return acc.reshape(B, T, D)


def decoder_layer(x, prev_h, pos, s_ret, s_hg, s_rw, mem_k, mem_v, lb, lw):
    mixed, new_prev, s_ret, s_hg, s_rw = token_mixer(x, prev_h, pos, s_ret, s_hg, s_rw, lb, lw)
    x = layer_norm(DN_ALPHA * x + mixed, lw['ln1_g'], lw['ln1_b'])
    x = layer_norm(DN_ALPHA * x + cross_attend(x, mem_k, mem_v, lw['xa_wq'], lw['xa_wo']), lw['ln2_g'], lw['ln2_b'])
    x = layer_norm(DN_ALPHA * x + moe(x, lw['router_w'], lw['router_b'], lw['exp_wgu'], lw['exp_bgu'],
                                      lw['exp_wd'], lw['exp_bd']), lw['ln3_g'], lw['ln3_b'])
    return x, new_prev, s_ret, s_hg, s_rw


def setup_inputs(seed: int = 0) -> dict:
    key = jax.random.key(seed)
    ks = iter(jax.random.split(key, 64))
    nk = lambda: next(ks)
    D = D_MODEL
    ones_noise = lambda shape: _rand(nk(), shape, 0.02, 1.0)
    small = lambda shape, s=0.02: _rand(nk(), shape, s)
    w0_base = -6.0 + 5.0 * jnp.linspace(0.0, 1.0, RW_W) ** 0.85 + 0.5
    return {
        'x_prompt': jax.random.normal(nk(), (BATCH, SEQ, D), jnp.float32),
        'x_sample': jax.random.normal(nk(), (DEC_BATCH, DEC_SEQ, D), jnp.float32),
        'mem_prompt': jax.random.normal(nk(), (BATCH, N_MEM, D), jnp.float32),
        'cache_mem_k': jax.random.normal(nk(), (DEPTH, DEC_BATCH, N_MEM, XA_HEADS, XA_HD), jnp.float32),
        'cache_mem_v': jax.random.normal(nk(), (DEPTH, DEC_BATCH, N_MEM, XA_HEADS, XA_HD), jnp.float32),
        'state_ret': jax.random.normal(nk(), (DEPTH, DEC_BATCH, RET_HEADS, RET_DK, RET_DV), jnp.float32),
        'state_hgrn': _rand(nk(), (DEPTH, DEC_BATCH, HG_HEADS, HG_DK, HG_DV), 0.5),
        'state_rwkv': _rand(nk(), (DEPTH, DEC_BATCH, RW_HEADS, RW_HD, RW_HD), 0.5),
        'state_shift': jax.random.normal(nk(), (DEPTH, DEC_BATCH, D), jnp.float32),
        'ln_in_g': ones_noise((D,)),
        'ln_in_b': small((D,)),
        'w_in': _rand(nk(), (DEPTH, D, N_IN_COLS), D ** -0.5),
        'mu_shift': jax.random.uniform(nk(), (DEPTH, N_RW_COLS), jnp.float32),
        'ret_gn': ones_noise((DEPTH, RET_W)),
        'hg_lb': small((DEPTH, HG_QK), 0.1),
        'hg_gn': ones_noise((DEPTH, HG_W)),
        'rw_w0': w0_base[None, :] + small((DEPTH, RW_W), 0.05),
        'rw_w2': _rand(nk(), (DEPTH, RW_DECAY_LORA, RW_W), 0.1 * RW_DECAY_LORA ** -0.5),
        'rw_a0': small((DEPTH, RW_W), 0.1),
        'rw_a2': _rand(nk(), (DEPTH, RW_A_LORA, RW_W), 0.1 * RW_A_LORA ** -0.5),
        'rw_g2': _rand(nk(), (DEPTH, RW_GATE_LORA, RW_W), RW_GATE_LORA ** -0.5),
        'rw_kk': _rand(nk(), (DEPTH, RW_W), 0.05, 0.85),
        'rw_ka': _rand(nk(), (DEPTH, RW_W), 0.05, 1.0),
        'rw_rk': small((DEPTH, RW_HEADS, RW_HD), 0.1),
        'rw_lnx_g': ones_noise((DEPTH, RW_W)),
        'rw_lnx_b': small((DEPTH, RW_W)),
        'w_out': _rand(nk(), (DEPTH, D, D), DN_BETA * D ** -0.5),
        'ln1_g': ones_noise((DEPTH, D)),
        'ln1_b': small((DEPTH, D)),
        'xa_wq': _rand(nk(), (DEPTH, D, XA_W), D ** -0.5),
        'xa_wk': _rand(nk(), (DEPTH, D, XA_W), D ** -0.5),
        'xa_wv': _rand(nk(), (DEPTH, D, XA_W), DN_BETA * D ** -0.5),
        'xa_wo': _rand(nk(), (DEPTH, XA_W, D), DN_BETA * XA_W ** -0.5),
        'ln2_g': ones_noise((DEPTH, D)),
        'ln2_b': small((DEPTH, D)),
        'router_w': _rand(nk(), (DEPTH, D, N_EXP), D ** -0.5),
        'router_b': small((DEPTH, N_EXP), 0.01),
        'exp_wgu': _rand(nk(), (DEPTH, N_EXP, D, 2 * D_FF), D ** -0.5),
        'exp_bgu': small((DEPTH, N_EXP, 2 * D_FF), 0.01),
        'exp_wd': _rand(nk(), (DEPTH, N_EXP, D_FF, D), DN_BETA * D_FF ** -0.5),
        'exp_bd': small((DEPTH, N_EXP, D), 0.01),
        'ln3_g': ones_noise((DEPTH, D)),
        'ln3_b': small((DEPTH, D)),
    }


def reference(x_prompt, x_sample, mem_prompt, cache_mem_k, cache_mem_v, state_ret, state_hgrn, state_rwkv,
              state_shift, ln_in_g, ln_in_b, w_in, mu_shift, ret_gn, hg_lb, hg_gn, rw_w0, rw_w2, rw_a0, rw_a2,
              rw_g2, rw_kk, rw_ka, rw_rk, rw_lnx_g, rw_lnx_b, w_out, ln1_g, ln1_b, xa_wq, xa_wk, xa_wv, xa_wo,
              ln2_g, ln2_b, router_w, router_b, exp_wgu, exp_bgu, exp_wd, exp_bd, ln3_g, ln3_b):
    f32 = jnp.float32
    Bp, Tp, _ = x_prompt.shape
    lb_cum = jnp.cumsum(jax.nn.softmax(hg_lb.astype(f32), axis=0), axis=0)
    lower_bounds = lb_cum - lb_cum[:1]
    xp = layer_norm(x_prompt, ln_in_g, ln_in_b)
    xs = layer_norm(x_sample, ln_in_g, ln_in_b)
    pos_p = jnp.arange(Tp)
    pos_s = PAST_LEN + jnp.arange(x_sample.shape[1])
    zero_shift = jnp.zeros((Bp, D_MODEL), x_prompt.dtype)
    zero_ret = jnp.zeros((Bp, RET_HEADS, RET_DK, RET_DV), f32)
    zero_hg = jnp.zeros((Bp, HG_HEADS, HG_DK, HG_DV), f32)
    zero_rw = jnp.zeros((Bp, RW_HEADS, RW_HD, RW_HD), f32)
    p_ret, p_hg, p_rw, p_sh, p_mk, p_mv = [], [], [], [], [], []
    s_ret, s_hg, s_rw, s_sh = [], [], [], []
    for l in range(DEPTH):
        lw = {'w_in': w_in[l], 'mu_shift': mu_shift[l], 'ret_gn': ret_gn[l], 'hg_gn': hg_gn[l],
              'rw_w0': rw_w0[l], 'rw_w2': rw_w2[l], 'rw_a0': rw_a0[l], 'rw_a2': rw_a2[l], 'rw_g2': rw_g2[l],
              'rw_kk': rw_kk[l], 'rw_ka': rw_ka[l], 'rw_rk': rw_rk[l], 'rw_lnx_g': rw_lnx_g[l],
              'rw_lnx_b': rw_lnx_b[l], 'w_out': w_out[l], 'ln1_g': ln1_g[l], 'ln1_b': ln1_b[l],
              'xa_wq': xa_wq[l], 'xa_wo': xa_wo[l], 'ln2_g': ln2_g[l], 'ln2_b': ln2_b[l],
              'router_w': router_w[l], 'router_b': router_b[l], 'exp_wgu': exp_wgu[l], 'exp_bgu': exp_bgu[l],
              'exp_wd': exp_wd[l], 'exp_bd': exp_bd[l], 'ln3_g': ln3_g[l], 'ln3_b': ln3_b[l]}
        lb = lower_bounds[l]
        mk = (mem_prompt @ xa_wk[l]).reshape(Bp, N_MEM, XA_HEADS, XA_HD)
        mv = (mem_prompt @ xa_wv[l]).reshape(Bp, N_MEM, XA_HEADS, XA_HD)
        xp, sh, r_, g_, w_ = decoder_layer(xp, zero_shift, pos_p, zero_ret, zero_hg, zero_rw, mk, mv, lb, lw)
        p_ret.append(r_); p_hg.append(g_); p_rw.append(w_); p_sh.append(sh); p_mk.append(mk); p_mv.append(mv)
        xs, sh, r_, g_, w_ = decoder_layer(xs, state_shift[l], pos_s, state_ret[l], state_hgrn[l], state_rwkv[l],
                                           cache_mem_k[l], cache_mem_v[l], lb, lw)
        s_ret.append(r_); s_hg.append(g_); s_rw.append(w_); s_sh.append(sh)
    return (xp, xs, jnp.stack(p_ret), jnp.stack(p_hg), jnp.stack(p_rw), jnp.stack(p_sh), jnp.stack(p_mk),
            jnp.stack(p_mv), jnp.stack(s_ret), jnp.stack(s_hg), jnp.stack(s_rw), jnp.stack(s_sh))
```

```python
import functools
import math

import jax
import jax.numpy as jnp
import numpy as np
from jax import lax
from jax.experimental import pallas as pl
from jax.experimental.pallas import tpu as pltpu

F32 = jnp.float32
BF16 = jnp.bfloat16

D_MODEL = 2048
PAST_LEN = 16384
XA_HEADS = 4
XA_HD = 128
XA_W = XA_HEADS * XA_HD
RET_DK = 64
RET_DV = 128
RET_HEADS = 4
RET_QK = RET_HEADS * RET_DK
RET_W = RET_HEADS * RET_DV
HG_DK = 128
HG_DV = 128
HG_HEADS = 4
HG_QK = HG_HEADS * HG_DK
HG_W = HG_HEADS * HG_DV
RW_HD = 64
RW_HEADS = 16
RW_W = RW_HEADS * RW_HD
RW_PAIRS = RW_W // 128
RW_DECAY_LORA = 64
RW_A_LORA = 64
RW_GATE_LORA = 128
RW_GN_EPS = 64e-5
N_MAIN_COLS = 2 * RET_QK + 2 * RET_W + 2 * HG_QK + 2 * HG_W
N_RW_COLS = 3 * RW_W + RW_DECAY_LORA + RW_A_LORA + RW_GATE_LORA
N_IN_COLS = N_MAIN_COLS + N_RW_COLS
ROPE_BASE = 10000.0
TOP_K = 4
SWIGLU_LIMIT = 7.0
SWIGLU_ALPHA = 1.702
DN_ALPHA = (2.0 * 4) ** 0.25
LN_EPS = 1e-5

C_XR = 0
C_XK = C_XR + RW_W
C_XV = C_XK + RW_W
C_RQ = C_XV + RW_W
C_RK = C_RQ + RET_QK
C_RV = C_RK + RET_QK
C_RG = C_RV + RET_W
C_HQ = C_RG + RET_W
C_HF = C_HQ + HG_QK
C_HI = C_HF + HG_QK
C_HG = C_HI + HG_W
C_XWA = C_HG + HG_W
C_XG = C_XWA + 128
assert C_XG + RW_GATE_LORA == N_IN_COLS

RET_CHUNK = 256
HG_CHUNK = 64
HG_SUB = 16
HG_EXP_CLAMP = 80.0
RW_CHUNK = 64
SAMPLE_BLOCK = 8
SAMPLE_STATE_BLOCK = 4

VMEM_LIMIT = 56 * 1024 * 1024


def _cparams(sem):
    return pltpu.CompilerParams(dimension_semantics=sem, vmem_limit_bytes=VMEM_LIMIT)


def _bdot(a, b):
    return jnp.dot(a.astype(BF16), b.astype(BF16), preferred_element_type=F32)


def _bdot_nt(a, b):
    return lax.dot_general(a.astype(BF16), b.astype(BF16), (((1,), (1,)), ((), ())),
                           preferred_element_type=F32)


def _bdot_tn(a, b):
    return lax.dot_general(a.astype(BF16), b.astype(BF16), (((0,), (0,)), ((), ())),
                           preferred_element_type=F32)


def _split_bf16(x, n):
    parts = []
    r = x
    for i in range(n):
        p = r.astype(BF16)
        parts.append(p)
        if i + 1 < n:
            r = r - p.astype(F32)
    return parts


def _dot_exact_lhs(m_bf16, x, n=3):
    return sum(jnp.dot(m_bf16, p, preferred_element_type=F32) for p in _split_bf16(x, n))


def _dot_exact_rhs(x, m_bf16, n=2):
    return sum(jnp.dot(p, m_bf16, preferred_element_type=F32) for p in _split_bf16(x, n))


def _dot3(a, b):
    ah, al = _split_bf16(a, 2)
    bh, bl = _split_bf16(b, 2)
    d = lambda x, y: jnp.dot(x, y, preferred_element_type=F32)
    return d(ah, bh) + (d(ah, bl) + d(al, bh))


def _sigmoid(x):
    return jax.nn.sigmoid(x)


def _silu(x):
    return x * jax.nn.sigmoid(x)


def _log_sigmoid(x):
    return jnp.minimum(x, 0.0) - jnp.log1p(jnp.exp(-jnp.abs(x)))


def _softplus(x):
    return jnp.maximum(x, 0.0) + jnp.log1p(jnp.exp(-jnp.abs(x)))


def _layer_norm(x, g, b):
    mu = jnp.mean(x, axis=-1, keepdims=True)
    d = x - mu
    var = jnp.mean(d * d, axis=-1, keepdims=True)
    return d * lax.rsqrt(var + LN_EPS) * g + b


def _tri_incl(c):
    r = lax.broadcasted_iota(jnp.int32, (c, c), 0)
    s = lax.broadcasted_iota(jnp.int32, (c, c), 1)
    return s <= r


def _seg_sum(x, seg):
    rows, width = x.shape
    nb = width // 128
    li = lax.broadcasted_iota(jnp.int32, (128, 128), 0) // seg
    lj = lax.broadcasted_iota(jnp.int32, (128, 128), 1) // seg
    m = jnp.where(li == lj, 1.0, 0.0).astype(BF16)
    xs = jnp.concatenate([x[:, i * 128:(i + 1) * 128] for i in range(nb)], axis=0)
    s = _dot_exact_rhs(xs, m, 3)
    return jnp.concatenate([s[i * rows:(i + 1) * rows] for i in range(nb)], axis=1)


def _head_rms(o, width):
    outs = []
    for h in range(o.shape[1] // width):
        oh = o[:, h * width:(h + 1) * width]
        ms = jnp.mean(oh * oh, axis=-1, keepdims=True)
        outs.append(oh * lax.rsqrt(ms + 1e-6))
    return jnp.concatenate(outs, axis=1)


def _rotary(x, cos, sin):
    x1 = x[:, :128]
    x2 = x[:, 128:]
    return jnp.concatenate([x1 * cos - x2 * sin, x1 * sin + x2 * cos], axis=1)


def _ln_kernel(x_ref, g_ref, b_ref, o_ref):
    o_ref[...] = _layer_norm(x_ref[...], g_ref[...], b_ref[...])


def _layer_norm_rows(x, g, b, tm):
    n, d = x.shape
    return pl.pallas_call(
        _ln_kernel,
        grid=(n // tm,),
        in_specs=[pl.BlockSpec((tm, d), lambda i: (i, 0)),
                  pl.BlockSpec((1, d), lambda i: (0, 0)),
                  pl.BlockSpec((1, d), lambda i: (0, 0))],
        out_specs=pl.BlockSpec((tm, d), lambda i: (i, 0)),
        out_shape=jax.ShapeDtypeStruct((n, d), F32),
        compiler_params=_cparams(("parallel",)),
        name="layer_norm",
    )(x, g.reshape(1, d), b.reshape(1, d))


def _add_ln_kernel(x_ref, y_ref, g_ref, b_ref, o_ref):
    o_ref[...] = _layer_norm(DN_ALPHA * x_ref[...] + y_ref[...], g_ref[...], b_ref[...])


def _add_layer_norm(x, y, g, b, tm):
    n, d = x.shape
    return pl.pallas_call(
        _add_ln_kernel,
        grid=(n // tm,),
        in_specs=[pl.BlockSpec((tm, d), lambda i: (i, 0)),
                  pl.BlockSpec((tm, d), lambda i: (i, 0)),
                  pl.BlockSpec((1, d), lambda i: (0, 0)),
                  pl.BlockSpec((1, d), lambda i: (0, 0))],
        out_specs=pl.BlockSpec((tm, d), lambda i: (i, 0)),
        out_shape=jax.ShapeDtypeStruct((n, d), F32),
        compiler_params=_cparams(("parallel",)),
        name="add_layer_norm",
    )(x, y, g.reshape(1, d), b.reshape(1, d))


def _mm_kernel(x_ref, w_ref, o_ref):
    o_ref[...] = _bdot(x_ref[...], w_ref[...])


def _matmul(x, w, layer, tm, tn, name):
    m, k = x.shape
    n = w.shape[2]
    return pl.pallas_call(
        _mm_kernel,
        grid=(m // tm, n // tn),
        in_specs=[pl.BlockSpec((tm, k), lambda i, j: (i, 0)),
                  pl.BlockSpec((None, k, tn), lambda i, j: (layer, 0, j))],
        out_specs=pl.BlockSpec((tm, tn), lambda i, j: (i, j)),
        out_shape=jax.ShapeDtypeStruct((m, n), F32),
        compiler_params=_cparams(("parallel", "parallel")),
        name=name,
    )(x, w)


def _ret_prompt_kernel(q_ref, k_ref, v_ref, g_ref, cos_ref, sin_ref, qdec_ref, kdec_ref, dmat_ref,
                       sdec_ref, smask_ref, gn_ref, o_ref, s_ref):
    ci = pl.program_id(1)

    @pl.when(ci == 0)
    def _():
        s_ref[...] = jnp.zeros_like(s_ref)

    cos = cos_ref[...]
    sin = sin_ref[...]
    q = _rotary(q_ref[...], cos, sin)
    k = _rotary(k_ref[...], cos, sin) * RET_DK ** -0.5
    v = v_ref[...]
    s0 = s_ref[...]
    inter = _bdot(q * qdec_ref[...], s0)
    lane = lax.broadcasted_iota(jnp.int32, (1, RET_QK), 1)
    head_of_lane = (lane % 128) // (RET_DK // 2)
    outs = []
    for h in range(RET_HEADS):
        qh = jnp.where(head_of_lane == h, q, 0.0)
        scores = _bdot_nt(qh, k) * dmat_ref[h]
        vh = v[:, h * RET_DV:(h + 1) * RET_DV]
        outs.append(_bdot(scores, vh) + inter[:, h * RET_DV:(h + 1) * RET_DV])
    s_ref[...] = s0 * sdec_ref[...] + smask_ref[...] * _bdot_tn(k * kdec_ref[...], v)
    o = jnp.concatenate(outs, axis=1)
    o_ref[...] = _head_rms(o, RET_DV) * gn_ref[...] * _silu(g_ref[...])


def _ret_tables(seq):
    c = min(RET_CHUNK, seq)
    lg_h = jnp.log1p(-jnp.exp2(-5.0 - jnp.arange(RET_HEADS, dtype=F32)))
    t = jnp.arange(c, dtype=F32)
    lag = t[:, None] - t[None, :]
    causal = lag >= 0
    dmat = jnp.where(causal, jnp.exp(jnp.where(causal, lag, 0.0)[None] * lg_h[:, None, None]), 0.0)
    head_of_lane = (np.arange(RET_QK) % 128) // (RET_DK // 2)
    lg_lane = lg_h[head_of_lane]
    qdec = jnp.exp((t + 1.0)[:, None] * lg_lane[None, :])
    kdec = jnp.exp((c - 1.0 - t)[:, None] * lg_lane[None, :])
    head_of_col = np.arange(RET_W) // RET_DV
    smask = jnp.asarray((head_of_lane[:, None] == head_of_col[None, :]).astype(np.float32))
    sdec = jnp.exp(c * lg_h)[head_of_col][None, :] * smask
    return c, dmat, qdec, kdec, sdec, smask


def _ret_prompt(proj, n_all, gn, cos, sin, tables, batch, seq):
    c, dmat, qdec, kdec, sdec, smask = tables
    nc = seq // c
    row = lambda b, ci: b * nc + ci
    const2 = lambda b, ci: (0, 0)
    return pl.pallas_call(
        _ret_prompt_kernel,
        grid=(batch, nc),
        in_specs=[pl.BlockSpec((c, RET_QK), lambda b, ci: (row(b, ci), C_RQ // RET_QK)),
                  pl.BlockSpec((c, RET_QK), lambda b, ci: (row(b, ci), C_RK // RET_QK)),
                  pl.BlockSpec((c, RET_W), lambda b, ci: (row(b, ci), C_RV // RET_W)),
                  pl.BlockSpec((c, RET_W), lambda b, ci: (row(b, ci), C_RG // RET_W)),
                  pl.BlockSpec((c, 128), lambda b, ci: (ci, 0)),
                  pl.BlockSpec((c, 128), lambda b, ci: (ci, 0)),
                  pl.BlockSpec((c, RET_QK), const2),
                  pl.BlockSpec((c, RET_QK), const2),
                  pl.BlockSpec((RET_HEADS, c, c), lambda b, ci: (0, 0, 0)),
                  pl.BlockSpec((RET_QK, RET_W), const2),
                  pl.BlockSpec((RET_QK, RET_W), const2),
                  pl.BlockSpec((1, RET_W), const2)],
        out_specs=[pl.BlockSpec((c, RET_W), lambda b, ci: (row(b, ci), 0)),
                   pl.BlockSpec((None, RET_QK, RET_W), lambda b, ci: (b, 0, 0))],
        out_shape=[jax.ShapeDtypeStruct((n_all, D_MODEL), F32),
                   jax.ShapeDtypeStruct((batch, RET_QK, RET_W), F32)],
        compiler_params=_cparams(("parallel", "arbitrary")),
        name="retention_prompt",
    )(proj, proj, proj, proj, cos, sin, qdec, kdec, dmat, sdec, smask, gn)


def _lower_bound(lb_param, layer):
    m = jnp.max(lb_param, axis=0, keepdims=True)
    e = jnp.exp(lb_param - m)
    rows = lax.broadcasted_iota(jnp.int32, e.shape, 0)
    picked = jnp.where((rows >= 1) & (rows <= layer), e, 0.0)
    return jnp.sum(picked, axis=0, keepdims=True) / jnp.sum(e, axis=0, keepdims=True)


def _hg_gates(hf, lb_param, layer):
    ls = _log_sigmoid(hf)
    if layer == 0:
        return ls, _sigmoid(-hf)
    lb = _lower_bound(lb_param, layer)
    a = jnp.log(lb)
    b = jnp.log1p(-lb) + ls
    log_f = jnp.maximum(a, b) + jnp.log1p(jnp.exp(-jnp.abs(a - b)))
    return log_f, (1.0 - lb) * _sigmoid(-hf)


def _hg_prompt_kernel(hq_ref, hf_ref, hi_ref, hg_ref, lb_ref, gn_ref, mix_hbm, o_ref, s_ref, *, layer):
    del mix_hbm
    ci = pl.program_id(1)
    c = hq_ref.shape[0]

    @pl.when(ci == 0)
    def _():
        s_ref[...] = jnp.zeros_like(s_ref)

    q = _silu(hq_ref[...])
    v = hi_ref[...]
    log_f, k = _hg_gates(hf_ref[...], lb_ref[...], layer)
    incl = _tri_incl(c)
    b = _dot_exact_lhs(jnp.where(incl, 1.0, 0.0).astype(BF16), log_f)
    b_end = b[c - 1:c]
    qe = q * jnp.exp(b)
    kd = k * jnp.exp(b_end - b)
    e_end = jnp.exp(b_end)
    outs = []
    for h in range(HG_HEADS):
        sl = slice(h * HG_DK, (h + 1) * HG_DK)
        bh, qh, kh, vh = b[:, sl], q[:, sl], k[:, sl], v[:, sl]
        rows = []
        for i in range(c // HG_SUB):
            r0 = i * HG_SUB
            ref = bh[r0:r0 + 1]
            qi = qh[r0:r0 + HG_SUB] * jnp.exp(bh[r0:r0 + HG_SUB] - ref)
            ki = kh * jnp.exp(jnp.minimum(ref - bh, HG_EXP_CLAMP))
            rows.append(_bdot_nt(qi, ki))
        a = jnp.where(incl, jnp.concatenate(rows, axis=0), 0.0)
        s0 = s_ref[h]
        o = _bdot(a, vh) + _bdot_nt(qe[:, sl], s0)
        s_ref[h] = s0 * e_end[:, sl] + _bdot_tn(vh, kd[:, sl])
        outs.append(o)
    o = jnp.concatenate(outs, axis=1)
    o_ref[...] = _head_rms(o, HG_DV) * gn_ref[...] * _silu(hg_ref[...])


def _hg_prompt(proj, mix, hg_lb, gn, layer, batch, seq):
    c = min(HG_CHUNK, seq)
    nc = seq // c
    row = lambda b, ci: b * nc + ci
    depth = hg_lb.shape[0]
    return pl.pallas_call(
        functools.partial(_hg_prompt_kernel, layer=layer),
        grid=(batch, nc),
        in_specs=[pl.BlockSpec((c, HG_QK), lambda b, ci: (row(b, ci), C_HQ // HG_QK)),
                  pl.BlockSpec((c, HG_QK), lambda b, ci: (row(b, ci), C_HF // HG_QK)),
                  pl.BlockSpec((c, HG_W), lambda b, ci: (row(b, ci), C_HI // HG_W)),
                  pl.BlockSpec((c, HG_W), lambda b, ci: (row(b, ci), C_HG // HG_W)),
                  pl.BlockSpec((depth, HG_QK), lambda b, ci: (0, 0)),
                  pl.BlockSpec((1, HG_W), lambda b, ci: (0, 0)),
                  pl.BlockSpec(memory_space=pl.ANY)],
        out_specs=[pl.BlockSpec((c, HG_W), lambda b, ci: (row(b, ci), RET_W // HG_W)),
                   pl.BlockSpec((None, HG_HEADS, HG_DV, HG_DK), lambda b, ci: (b, 0, 0, 0))],
        out_shape=[jax.ShapeDtypeStruct(mix.shape, F32),
                   jax.ShapeDtypeStruct((batch, HG_HEADS, HG_DV, HG_DK), F32)],
        input_output_aliases={6: 0},
        compiler_params=_cparams(("parallel", "arbitrary")),
        name="hgrn2_prompt",
    )(proj, proj, proj, proj, hg_lb, gn, mix)


def _rw_prep(xr, xk, xv, xwa, xg, w0, w2p, a0, a2p, g2, kkw, kaw, rkw):
    lane = lax.broadcasted_iota(jnp.int32, (1, 128), 1)
    lo = lane < RW_DECAY_LORA
    xw_part = jnp.where(lo, jnp.tanh(xwa), 0.0)
    xa_part = jnp.where(lo, 0.0, xwa)
    w_log = -_softplus(-(w0 + _bdot(xw_part, w2p))) - 0.5
    log_w = -jnp.exp(w_log)
    a = _sigmoid(a0 + _bdot(xa_part, a2p))
    g = _bdot(_sigmoid(xg), g2)
    kk = xk * kkw
    kappa = kk / jnp.maximum(jnp.sqrt(_seg_sum(kk * kk, RW_HD)), 1e-12)
    kr = xk * (1.0 + (a - 1.0) * kaw)
    bonus = _seg_sum(xr * kr * rkw, RW_HD) * xv
    return log_w, a, g, kappa, kr, bonus


def _rw_post(y, bonus, g, ln_g, ln_b):
    mu = _seg_sum(y, RW_HD) * (1.0 / RW_HD)
    d = y - mu
    var = _seg_sum(d * d, RW_HD) * (1.0 / RW_HD)
    yn = d * lax.rsqrt(var + RW_GN_EPS) * ln_g + ln_b
    return (yn + bonus) * g


def _unit_lower_inverse(n, size, chunk):
    r = lax.broadcasted_iota(jnp.int32, (size, size), 0)
    s = lax.broadcasted_iota(jnp.int32, (size, size), 1)
    x = jnp.where(r == s, 1.0, 0.0) - n
    p = _dot3(n, n)
    power = 2
    while True:
        x = x + _dot3(x, p)
        power *= 2
        if power >= chunk:
            break
        p = _dot3(p, p)
    return x


def _rw_pair_chunk(rt, kt, kh, ah, kd, ad, v, s0, wc):
    c = rt.shape[0]
    lane = lax.broadcasted_iota(jnp.int32, (1, 128), 1)
    m0 = lane < RW_HD

    def stack(x):
        return jnp.concatenate([jnp.where(m0, x, 0.0), jnp.where(m0, 0.0, x)], axis=0)

    k2, r2, kh2, ah2, kd2, ad2, v2 = (stack(x) for x in (kt, rt, kh, ah, kd, ad, v))
    kr2 = jnp.concatenate([k2, r2], axis=0)
    gram = _bdot_nt(kr2, jnp.concatenate([kh2, ah2], axis=0))
    n2 = 2 * c
    ti = lax.broadcasted_iota(jnp.int32, (n2, n2), 0) % c
    si = lax.broadcasted_iota(jnp.int32, (n2, n2), 1) % c
    strict = si < ti
    incl = si <= ti
    a_kk = jnp.where(strict, gram[:n2, :n2], 0.0)
    a_ka = jnp.where(strict, gram[:n2, n2:], 0.0)
    a_rk = jnp.where(incl, gram[n2:, :n2], 0.0)
    a_ra = jnp.where(incl, gram[n2:, n2:], 0.0)
    inv = _unit_lower_inverse(a_ka, n2, c)
    from_state = _bdot_nt(kr2, s0)
    u = _dot3(inv, from_state[:n2] + _bdot(a_kk, v2))
    y2 = from_state[n2:] + _bdot(jnp.concatenate([a_rk, -a_ra], axis=1), jnp.concatenate([v2, u], axis=0))
    s_new = s0 * wc + _bdot_tn(jnp.concatenate([v2, -u], axis=0), jnp.concatenate([kd2, ad2], axis=0))
    return y2[:c] + y2[c:], s_new


def _rw_prompt_kernel(xr_ref, xk_ref, xv_ref, xwa_ref, xg_ref, mur_ref, muk_ref, muv_ref, muwa_ref, mug_ref,
                      w0_ref, w2p_ref, a0_ref, a2p_ref, g2_ref, kkw_ref, kaw_ref, rkw_ref, lng_ref, lnb_ref,
                      mix_hbm, o_ref, s_ref, cr_ref, ck_ref, cv_ref, cwa_ref, cg_ref):
    del mix_hbm
    ci = pl.program_id(1)
    c = xr_ref.shape[0]

    @pl.when(ci == 0)
    def _():
        s_ref[...] = jnp.zeros_like(s_ref)
        for ref in (cr_ref, ck_ref, cv_ref, cwa_ref, cg_ref):
            ref[...] = jnp.zeros_like(ref)

    def shift_mix(x_ref, carry_ref, mu_ref):
        x = x_ref[...]
        rows = lax.broadcasted_iota(jnp.int32, x.shape, 0)
        prev = jnp.where(rows == 0, carry_ref[0:1, :], pltpu.roll(x, 1, 0))
        carry_ref[0:1, :] = x[c - 1:c, :]
        return x + mu_ref[...] * (prev - x)

    xr = shift_mix(xr_ref, cr_ref, mur_ref)
    xk = shift_mix(xk_ref, ck_ref, muk_ref)
    xv = shift_mix(xv_ref, cv_ref, muv_ref)
    xwa = shift_mix(xwa_ref, cwa_ref, muwa_ref)
    xg = shift_mix(xg_ref, cg_ref, mug_ref)
    log_w, a, g, kappa, kr, bonus = _rw_prep(xr, xk, xv, xwa, xg, w0_ref[...], w2p_ref[...], a0_ref[...],
                                             a2p_ref[...], g2_ref[...], kkw_ref[...], kaw_ref[...], rkw_ref[...])
    atil = kappa * a
    cum = _dot_exact_lhs(jnp.where(_tri_incl(c), 1.0, 0.0).astype(BF16), log_w)
    cum_end = cum[c - 1:c]
    e_neg = jnp.exp(-cum)
    e_end = jnp.exp(cum_end - cum)
    rt = xr * jnp.exp(cum)
    kt = kappa * jnp.exp(cum - log_w)
    kh = kr * e_neg
    ah = atil * e_neg
    kd = kr * e_end
    ad = atil * e_end
    wc = jnp.exp(cum_end)
    ys = []
    for p in range(RW_PAIRS):
        sl = slice(p * 128, (p + 1) * 128)
        y, s_new = _rw_pair_chunk(rt[:, sl], kt[:, sl], kh[:, sl], ah[:, sl], kd[:, sl], ad[:, sl], xv[:, sl],
                                  s_ref[p], wc[:, sl])
        s_ref[p] = s_new
        ys.append(y)
    o_ref[...] = _rw_post(jnp.concatenate(ys, axis=1), bonus, g, lng_ref[...], lnb_ref[...])


def _rw_prompt(proj, mix, mu, rwp, batch, seq):
    c = min(RW_CHUNK, seq)
    nc = seq // c
    row = lambda b, ci: b * nc + ci
    const = lambda b, ci: (0, 0)
    wide = lambda col: pl.BlockSpec((c, RW_W), lambda b, ci: (row(b, ci), col // RW_W))
    narrow = lambda col: pl.BlockSpec((c, 128), lambda b, ci: (row(b, ci), col // 128))
    vec = lambda n: pl.BlockSpec((1, n), const)
    return pl.pallas_call(
        _rw_prompt_kernel,
        grid=(batch, nc),
        in_specs=[wide(C_XR), wide(C_XK), wide(C_XV), narrow(C_XWA), narrow(C_XG),
                  vec(RW_W), vec(RW_W), vec(RW_W), vec(128), vec(128),
                  vec(RW_W), pl.BlockSpec((128, RW_W), const), vec(RW_W), pl.BlockSpec((128, RW_W), const),
                  pl.BlockSpec((RW_GATE_LORA, RW_W), const), vec(RW_W), vec(RW_W), vec(RW_W), vec(RW_W), vec(RW_W),
                  pl.BlockSpec(memory_space=pl.ANY)],
        out_specs=[pl.BlockSpec((c, RW_W), lambda b, ci: (row(b, ci), (RET_W + HG_W) // RW_W)),
                   pl.BlockSpec((None, RW_PAIRS, 128, 128), lambda b, ci: (b, 0, 0, 0))],
        out_shape=[jax.ShapeDtypeStruct(mix.shape, F32),
                   jax.ShapeDtypeStruct((batch, RW_PAIRS, 128, 128), F32)],
        scratch_shapes=[pltpu.VMEM((8, RW_W), F32), pltpu.VMEM((8, RW_W), F32), pltpu.VMEM((8, RW_W), F32),
                        pltpu.VMEM((8, 128), F32), pltpu.VMEM((8, 128), F32)],
        input_output_aliases={20: 0},
        compiler_params=_cparams(("parallel", "arbitrary")),
        name="rwkv7_prompt",
    )(proj, proj, proj, proj, proj, mu["r"], mu["k"], mu["v"], mu["wa"], mu["g"],
      rwp["w0"], rwp["w2p"], rwp["a0"], rwp["a2p"], rwp["g2"], rwp["kk"], rwp["ka"], rwp["rk"],
      rwp["lng"], rwp["lnb"], mix)


def _sample_prep_kernel(xr_ref, xk_ref, xv_ref, xwa_ref, xg_ref, pr_ref, pk_ref, pv_ref, pwa_ref, pg_ref,
                        mur_ref, muk_ref, muv_ref, muwa_ref, mug_ref,
                        w0_ref, w2p_ref, a0_ref, a2p_ref, g2_ref, kkw_ref, kaw_ref, rkw_ref,
                        rq_ref, rk_ref, cos_ref, sin_ref, hq_ref, hf_ref, lb_ref,
                        r_o, w_o, kr_o, v_o, kap_o, at_o, g_o, bonus_o, q_o, k_o, hq_o, hk_o, heg_o, *, layer):
    mix = lambda x_ref, p_ref, mu_ref: x_ref[...] + mu_ref[...] * (p_ref[...] - x_ref[...])
    xr = mix(xr_ref, pr_ref, mur_ref)
    xk = mix(xk_ref, pk_ref, muk_ref)
    xv = mix(xv_ref, pv_ref, muv_ref)
    xwa = mix(xwa_ref, pwa_ref, muwa_ref)
    xg = mix(xg_ref, pg_ref, mug_ref)
    log_w, a, g, kappa, kr, bonus = _rw_prep(xr, xk, xv, xwa, xg, w0_ref[...], w2p_ref[...], a0_ref[...],
                                             a2p_ref[...], g2_ref[...], kkw_ref[...], kaw_ref[...], rkw_ref[...])
    r_o[...] = xr
    w_o[...] = jnp.exp(log_w)
    kr_o[...] = kr
    v_o[...] = xv
    kap_o[...] = kappa
    at_o[...] = kappa * a
    g_o[...] = g
    bonus_o[...] = bonus
    cos = cos_ref[...]
    sin = sin_ref[...]
    q_o[...] = _rotary(rq_ref[...], cos, sin)
    k_o[...] = _rotary(rk_ref[...], cos, sin) * RET_DK ** -0.5
    log_f, k_in = _hg_gates(hf_ref[...], lb_ref[...], layer)
    hq_o[...] = _silu(hq_ref[...])
    hk_o[...] = k_in
    heg_o[...] = jnp.exp(log_f)


def _sample_prep(proj, proj_shift, mu, rwp, cos_s, sin_s, hg_lb, layer, n_prompt, nb):
    rb = n_prompt // nb
    blk = lambda w, col, r: pl.BlockSpec((nb, w), lambda i: (r, col // w))
    full = lambda a: pl.BlockSpec(a.shape, lambda i: (0,) * a.ndim)
    wide = jax.ShapeDtypeStruct((nb, RW_W), F32)
    consts = [mu["r"], mu["k"], mu["v"], mu["wa"], mu["g"], rwp["w0"], rwp["w2p"], rwp["a0"], rwp["a2p"],
              rwp["g2"], rwp["kk"], rwp["ka"], rwp["rk"]]
    return pl.pallas_call(
        functools.partial(_sample_prep_kernel, layer=layer),
        grid=(1,),
        in_specs=[blk(RW_W, C_XR, rb), blk(RW_W, C_XK, rb), blk(RW_W, C_XV, rb), blk(128, C_XWA, rb), blk(128, C_XG, rb),
                  blk(RW_W, C_XR, 0), blk(RW_W, C_XK, 0), blk(RW_W, C_XV, 0), blk(128, C_XWA, 0), blk(128, C_XG, 0)]
                 + [full(a) for a in consts]
                 + [blk(RET_QK, C_RQ, rb), blk(RET_QK, C_RK, rb), full(cos_s), full(sin_s),
                    blk(HG_QK, C_HQ, rb), blk(HG_QK, C_HF, rb), full(hg_lb)],
        out_specs=[pl.BlockSpec((nb, RW_W), lambda i: (0, 0))] * 8
                  + [pl.BlockSpec((nb, RET_QK), lambda i: (0, 0))] * 2
                  + [pl.BlockSpec((nb, HG_QK), lambda i: (0, 0))] * 3,
        out_shape=[wide] * 8 + [jax.ShapeDtypeStruct((nb, RET_QK), F32)] * 2
                  + [jax.ShapeDtypeStruct((nb, HG_QK), F32)] * 3,
        compiler_params=pltpu.CompilerParams(vmem_limit_bytes=VMEM_LIMIT),
        name="sample_prep",
    )(proj, proj, proj, proj, proj, proj_shift, proj_shift, proj_shift, proj_shift, proj_shift,
      *consts, proj, proj, cos_s, sin_s, proj, proj, hg_lb)


def _sample_state_kernel(*refs, aliased):
    (sr_ref, sh_ref, sw_ref, gam_ref, rq_ref, rk_ref, rv_ref, hq_ref, hk_ref, heg_ref, hv_ref,
     wr_ref, ww_ref, wk_ref, wv_ref, wkap_ref, wat_ref) = refs[:17]
    outs = refs[17 + (3 if aliased else 0):]
    nsr_ref, nsh_ref, nsw_ref, oret_ref, ohg_ref, yrw_ref = outs
    s = gam_ref[...] * sr_ref[...] + rk_ref[...] * rv_ref[...]
    nsr_ref[...] = s
    oret_ref[...] = jnp.sum(rq_ref[...] * s, axis=2, keepdims=True)
    s = sh_ref[...] * heg_ref[...] + hk_ref[...] * hv_ref[...]
    nsh_ref[...] = s
    ohg_ref[...] = jnp.sum(hq_ref[...] * s, axis=2, keepdims=True)
    s = sw_ref[...]
    sa = -jnp.sum(s * wkap_ref[...], axis=3, keepdims=True)
    s = s * ww_ref[...] + sa * wat_ref[...] + wv_ref[...] * wk_ref[...]
    nsw_ref[...] = s
    yrw_ref[...] = jnp.sum(s * wr_ref[...], axis=3, keepdims=True)


def _sample_state(state_ret, state_hgrn, state_rwkv, prev_new, layer, gam, cols):
    depth, nb = state_ret.shape[:2]
    bb = SAMPLE_STATE_BLOCK
    st = lambda a: pl.BlockSpec((None, bb) + a.shape[2:], lambda i: (layer, i, 0, 0, 0))
    per_b = lambda a: pl.BlockSpec((bb,) + a.shape[1:], lambda i: (i, 0, 0, 0))
    aliased = prev_new is not None
    ins = [state_ret, state_hgrn, state_rwkv, gam] + list(cols)
    in_specs = [st(state_ret), st(state_hgrn), st(state_rwkv), pl.BlockSpec(gam.shape, lambda i: (0, 0, 0, 0))]
    in_specs += [per_b(a) for a in cols]
    io_alias = {}
    if aliased:
        io_alias = {len(ins) + j: j for j in range(3)}
        ins += list(prev_new)
        in_specs += [pl.BlockSpec(memory_space=pl.ANY)] * 3
    out_shape = [jax.ShapeDtypeStruct(state_ret.shape, F32), jax.ShapeDtypeStruct(state_hgrn.shape, F32),
                 jax.ShapeDtypeStruct(state_rwkv.shape, F32),
                 jax.ShapeDtypeStruct((nb, RET_HEADS, 1, RET_DV), F32),
                 jax.ShapeDtypeStruct((nb, HG_HEADS, 1, HG_DV), F32),
                 jax.ShapeDtypeStruct((nb, RW_HEADS, RW_HD, 1), F32)]
    out_specs = [st(state_ret), st(state_hgrn), st(state_rwkv)] + [per_b(a) for a in out_shape[3:]]
    return pl.pallas_call(
        functools.partial(_sample_state_kernel, aliased=aliased),
        grid=(nb // bb,),
        in_specs=in_specs,
        out_specs=out_specs,
        out_shape=out_shape,
        input_output_aliases=io_alias,
        compiler_params=_cparams(("parallel",)),
        name="sample_state",
    )(*ins)


def _sample_post_kernel(oret_ref, rg_ref, rgn_ref, ohg_ref, hg_ref, hgn_ref, y_ref, bonus_ref, g_ref,
                        lng_ref, lnb_ref, mix_hbm, o_ref):
    del mix_hbm
    out_a = _head_rms(oret_ref[...], RET_DV) * rgn_ref[...] * _silu(rg_ref[...])
    out_b = _head_rms(ohg_ref[...], HG_DV) * hgn_ref[...] * _silu(hg_ref[...])
    out_c = _rw_post(y_ref[...], bonus_ref[...], g_ref[...], lng_ref[...], lnb_ref[...])
    o_ref[...] = jnp.concatenate([out_a, out_b, out_c], axis=1)


def _sample_post(o_ret, o_hg, y_rw, bonus, g, proj, mix, ret_gn, hg_gn, rwp, n_prompt, nb):
    rb = n_prompt // nb
    full = lambda a: pl.BlockSpec(a.shape, lambda i: (0,) * a.ndim)
    ins = [o_ret, proj, ret_gn, o_hg, proj, hg_gn, y_rw, bonus, g, rwp["lng"], rwp["lnb"], mix]
    in_specs = [full(o_ret), pl.BlockSpec((nb, RET_W), lambda i: (rb, C_RG // RET_W)), full(ret_gn),
                full(o_hg), pl.BlockSpec((nb, HG_W), lambda i: (rb, C_HG // HG_W)), full(hg_gn),
                full(y_rw), full(bonus), full(g), full(rwp["lng"]), full(rwp["lnb"]),
                pl.BlockSpec(memory_space=pl.ANY)]
    return pl.pallas_call(
        _sample_post_kernel,
        grid=(1,),
        in_specs=in_specs,
        out_specs=pl.BlockSpec((nb, D_MODEL), lambda i: (rb, 0)),
        out_shape=jax.ShapeDtypeStruct(mix.shape, F32),
        input_output_aliases={11: 0},
        compiler_params=pltpu.CompilerParams(vmem_limit_bytes=VMEM_LIMIT),
        name="sample_post",
    )(*ins)


def _softmax_rows(s):
    m = jnp.max(s, axis=-1, keepdims=True)
    e = jnp.exp(s - m)
    return e / jnp.sum(e, axis=-1, keepdims=True)


def _xattn_prompt_kernel(q_ref, k_ref, v_ref, o_ref):
    q = q_ref[...]
    k = k_ref[...]
    v = v_ref[...]
    outs = []
    for h in range(XA_HEADS):
        sl = slice(h * XA_HD, (h + 1) * XA_HD)
        p = _softmax_rows(_bdot_nt(q[:, sl], k[:, sl]) * XA_HD ** -0.5)
        outs.append(_bdot(p, v[:, sl]))
    o_ref[...] = jnp.concatenate(outs, axis=1)


def _xattn_prompt(q_all, mem_k, mem_v, batch, seq, tq):
    n_mem = mem_k.shape[1]
    nq = seq // tq
    return pl.pallas_call(
        _xattn_prompt_kernel,
        grid=(batch, nq),
        in_specs=[pl.BlockSpec((tq, XA_W), lambda b, i: (b * nq + i, 0)),
                  pl.BlockSpec((None, n_mem, XA_W), lambda b, i: (b, 0, 0)),
                  pl.BlockSpec((None, n_mem, XA_W), lambda b, i: (b, 0, 0))],
        out_specs=pl.BlockSpec((tq, XA_W), lambda b, i: (b * nq + i, 0)),
        out_shape=jax.ShapeDtypeStruct(q_all.shape, F32),
        compiler_params=_cparams(("parallel", "parallel")),
        name="xattn_prompt",
    )(q_all, mem_k, mem_v)


def _xattn_sample_kernel(q_ref, k_ref, v_ref, att_hbm, o_ref):
    del att_hbm
    bb = q_ref.shape[0]
    head_of_lane = lax.broadcasted_iota(jnp.int32, (8, XA_W), 1) // XA_HD
    row = lax.broadcasted_iota(jnp.int32, (8, XA_W), 0)
    own = head_of_lane == row
    outs = []
    for b in range(bb):
        q8 = jnp.where(own, q_ref[b:b + 1, :], 0.0)
        p = _softmax_rows(_bdot_nt(q8, k_ref[b]) * XA_HD ** -0.5)
        o8 = _bdot(p, v_ref[b])
        outs.append(jnp.sum(jnp.where(own, o8, 0.0), axis=0, keepdims=True))
    o_ref[...] = jnp.concatenate(outs, axis=0)


def _xattn_sample(q_all, att, cache_k, cache_v, layer, n_prompt, nb):
    bb = SAMPLE_BLOCK
    n_mem = cache_k.shape[2]
    rb = n_prompt // bb
    return pl.pallas_call(
        _xattn_sample_kernel,
        grid=(nb // bb,),
        in_specs=[pl.BlockSpec((bb, XA_W), lambda i: (rb + i, 0)),
                  pl.BlockSpec((None, bb, n_mem, XA_W), lambda i: (layer, i, 0, 0)),
                  pl.BlockSpec((None, bb, n_mem, XA_W), lambda i: (layer, i, 0, 0)),
                  pl.BlockSpec(memory_space=pl.ANY)],
        out_specs=pl.BlockSpec((bb, XA_W), lambda i: (rb + i, 0)),
        out_shape=jax.ShapeDtypeStruct(att.shape, F32),
        input_output_aliases={3: 0},
        compiler_params=_cparams(("parallel",)),
        name="xattn_sample",
    )(q_all, cache_k, cache_v, att)


def _router_kernel(x_ref, w_ref, b_ref, o_ref):
    logits = _dot3(x_ref[...], w_ref[...]) + b_ref[...]
    n_exp = logits.shape[1]
    col = lax.broadcasted_iota(jnp.int32, logits.shape, 1)
    work = logits
    picks = []
    for _ in range(TOP_K):
        m = jnp.max(work, axis=-1, keepdims=True)
        idx = jnp.min(jnp.where(work == m, col, n_exp), axis=-1, keepdims=True)
        sel = col == idx
        picks.append((m, sel))
        work = jnp.where(sel, -jnp.inf, work)
    top = picks[0][0]
    es = [jnp.exp(m - top) for m, _ in picks]
    denom = es[0] + es[1] + es[2] + es[3]
    comb = jnp.zeros_like(logits)
    for e, (_, sel) in zip(es, picks):
        comb = comb + jnp.where(sel, e / denom, 0.0)
    o_ref[...] = comb


def _router(x, router_w, router_b, layer, tm):
    n, d = x.shape
    n_exp = router_w.shape[2]
    return pl.pallas_call(
        _router_kernel,
        grid=(n // tm,),
        in_specs=[pl.BlockSpec((tm, d), lambda i: (i, 0)),
                  pl.BlockSpec((None, d, n_exp), lambda i: (layer, 0, 0)),
                  pl.BlockSpec((None, 1, n_exp), lambda i: (layer, 0, 0))],
        out_specs=pl.BlockSpec((tm, n_exp), lambda i: (i, 0)),
        out_shape=jax.ShapeDtypeStruct((n, n_exp), F32),
        compiler_params=_cparams(("parallel",)),
        name="router",
    )(x, router_w, router_b)


def _moe_dense_kernel(x_ref, comb_ref, wg_ref, wu_ref, bg_ref, bu_ref, wd_ref, bd_ref, o_ref, y_ref):
    e = pl.program_id(1)
    j = pl.program_id(2)
    nj = pl.num_programs(2)

    @pl.when((e == 0) & (j == 0))
    def _():
        o_ref[...] = jnp.zeros_like(o_ref)

    @pl.when(j == 0)
    def _():
        y_ref[...] = jnp.zeros_like(y_ref)

    x = x_ref[...]
    gate = jnp.minimum(_bdot(x, wg_ref[...]) + bg_ref[...], SWIGLU_LIMIT)
    up = jnp.clip(_bdot(x, wu_ref[...]) + bu_ref[...], -SWIGLU_LIMIT, SWIGLU_LIMIT)
    hdn = (up + 1.0) * gate * _sigmoid(SWIGLU_ALPHA * gate)
    y_ref[...] += _bdot(hdn, wd_ref[...])

    @pl.when(j == nj - 1)
    def _():
        comb = comb_ref[...]
        col = lax.broadcasted_iota(jnp.int32, comb.shape, 1)
        c_e = jnp.sum(jnp.where(col == e, comb, 0.0), axis=-1, keepdims=True)
        o_ref[...] += c_e * (y_ref[...] + bd_ref[...])


def _moe_dense(x_bf16, comb, wgu, bgu, wd, bd, layer, tm, tf):
    n, d = x_bf16.shape
    n_exp = comb.shape[1]
    d_ff = wd.shape[2]
    nj = d_ff // tf
    return pl.pallas_call(
        _moe_dense_kernel,
        grid=(n // tm, n_exp, nj),
        in_specs=[pl.BlockSpec((tm, d), lambda i, e, j: (i, 0)),
                  pl.BlockSpec((tm, n_exp), lambda i, e, j: (i, 0)),
                  pl.BlockSpec((None, None, d, tf), lambda i, e, j: (layer, e, 0, j)),
                  pl.BlockSpec((None, None, d, tf), lambda i, e, j: (layer, e, 0, nj + j)),
                  pl.BlockSpec((None, None, 1, tf), lambda i, e, j: (layer, e, 0, j)),
                  pl.BlockSpec((None, None, 1, tf), lambda i, e, j: (layer, e, 0, nj + j)),
                  pl.BlockSpec((None, None, tf, d), lambda i, e, j: (layer, e, j, 0)),
                  pl.BlockSpec((None, None, 1, d), lambda i, e, j: (layer, e, 0, 0))],
        out_specs=pl.BlockSpec((tm, d), lambda i, e, j: (i, 0)),
        out_shape=jax.ShapeDtypeStruct((n, d), F32),
        scratch_shapes=[pltpu.VMEM((tm, d), F32)],
        compiler_params=_cparams(("parallel", "arbitrary", "arbitrary")),
        name="moe_dense",
    )(x_bf16, comb, wgu, wgu, bgu, bgu, wd, bd)


def _rot_perm(w):
    lead = w.shape[:-1]
    return w.reshape(lead + (RET_HEADS, 2, RET_DK // 2)).swapaxes(-3, -2).reshape(lead + (RET_QK,))


def _permute_in_columns(a):
    o = np.cumsum((0, RET_QK, RET_QK, RET_W, RET_W, HG_QK, HG_QK, HG_W, HG_W, RW_W, RW_DECAY_LORA, RW_W, RW_W,
                   RW_A_LORA, RW_GATE_LORA))
    seg = lambda i: a[..., o[i]:o[i + 1]]
    rq, rk, rv, rg, hq, hf, hi, hg, xr, xw, xk, xv, xa, xg = (seg(i) for i in range(14))
    return jnp.concatenate([xr, xk, xv, _rot_perm(rq), _rot_perm(rk), rv, rg, hq, hf, hi, hg, xw, xa, xg], axis=-1)


def _unpermute_ret_state(s):
    b = s.shape[0]
    s6 = s.reshape(b, 2, RET_HEADS, RET_DK // 2, RET_HEADS, RET_DV)
    per_head = jnp.stack([s6[:, :, h, :, h, :] for h in range(RET_HEADS)], axis=1)
    return per_head.reshape(b, RET_HEADS, RET_DK, RET_DV)


def _unpair_rw_state(s):
    b = s.shape[0]
    s6 = s.reshape(b, RW_PAIRS, 2, RW_HD, 2, RW_HD)
    per_head = jnp.stack([s6[:, :, x, :, x, :] for x in range(2)], axis=2)
    return per_head.reshape(b, RW_HEADS, RW_HD, RW_HD)


def _pick_tile(n, target, mult):
    t = min(n, target) // mult * mult
    while n % t:
        t -= mult
    return t


def kernel(x_prompt, x_sample, mem_prompt, cache_mem_k, cache_mem_v, state_ret, state_hgrn, state_rwkv, state_shift, ln_in_g, ln_in_b, w_in, mu_shift, ret_gn, hg_lb, hg_gn, rw_w0, rw_w2, rw_a0, rw_a2, rw_g2, rw_kk, rw_ka, rw_rk, rw_lnx_g, rw_lnx_b, w_out, ln1_g, ln1_b, xa_wq, xa_wk, xa_wv, xa_wo, ln2_g, ln2_b, router_w, router_b, exp_wgu, exp_bgu, exp_wd, exp_bd, ln3_g, ln3_b):
    batch, seq, d = x_prompt.shape
    nb = x_sample.shape[0]
    assert x_sample.shape[1] == 1 and d == D_MODEL
    depth = w_in.shape[0]
    n_exp = router_w.shape[2]
    n_mem = mem_prompt.shape[1]
    n_prompt = batch * seq
    n_all = n_prompt + nb
    assert n_prompt % nb == 0 and nb % SAMPLE_BLOCK == 0
    tm = _pick_tile(n_all, 640, 128)
    tm_moe = _pick_tile(n_all, 1040, 16)

    w_in_p = _permute_in_columns(w_in)
    pad = N_IN_COLS - N_RW_COLS
    mu_p = _permute_in_columns(jnp.concatenate([jnp.zeros((depth, pad), F32), mu_shift], axis=-1))
    zeros_lora = jnp.zeros((depth, 64, RW_W), F32)
    w2p = jnp.concatenate([rw_w2, zeros_lora], axis=1)
    a2p = jnp.concatenate([zeros_lora, rw_a2], axis=1)
    inv_freq = jnp.power(ROPE_BASE, -jnp.arange(RET_DK // 2, dtype=F32) / (RET_DK // 2))

    def rope_tables(pos):
        ang = pos.astype(F32)[:, None] * inv_freq[None, :]
        return jnp.tile(jnp.cos(ang), (1, RET_HEADS)), jnp.tile(jnp.sin(ang), (1, RET_HEADS))

    cos_p, sin_p = rope_tables(jnp.arange(seq))
    cos_s, sin_s = rope_tables(PAST_LEN + jnp.arange(1))
    ret_tab = _ret_tables(seq)
    gam = jnp.broadcast_to(jnp.exp(jnp.log1p(-jnp.exp2(-5.0 - jnp.arange(RET_HEADS, dtype=F32))))[None, :, None, None],
                           (1, RET_HEADS, 1, RET_DV))
    mem_flat = mem_prompt.reshape(batch * n_mem, d)
    cache_k = cache_mem_k.reshape(depth, nb, n_mem, XA_W)
    cache_v = cache_mem_v.reshape(depth, nb, n_mem, XA_W)
    bgu = exp_bgu.reshape(depth, n_exp, 1, exp_bgu.shape[-1])
    bd = exp_bd.reshape(depth, n_exp, 1, d)
    rb3 = router_b.reshape(depth, 1, n_exp)

    x_all = _layer_norm_rows(jnp.concatenate([x_prompt.reshape(n_prompt, d), x_sample.reshape(nb, d)], axis=0),
                             ln_in_g, ln_in_b, tm)

    p_ret, p_hg, p_rw, p_sh, p_mk, p_mv, s_sh = [], [], [], [], [], [], []
    new_sample_states = None
    for l in range(depth):
        mu = {"r": mu_p[l:l + 1, C_XR:C_XR + RW_W], "k": mu_p[l:l + 1, C_XK:C_XK + RW_W],
              "v": mu_p[l:l + 1, C_XV:C_XV + RW_W], "wa": mu_p[l:l + 1, C_XWA:C_XWA + 128],
              "g": mu_p[l:l + 1, C_XG:C_XG + 128]}
        rwp = {"w0": rw_w0[l:l + 1], "w2p": w2p[l], "a0": rw_a0[l:l + 1], "a2p": a2p[l], "g2": rw_g2[l],
               "kk": rw_kk[l:l + 1], "ka": rw_ka[l:l + 1], "rk": rw_rk[l].reshape(1, RW_W),
               "lng": rw_lnx_g[l:l + 1], "lnb": rw_lnx_b[l:l + 1]}
        p_sh.append(x_all[:n_prompt].reshape(batch, seq, d)[:, -1])
        s_sh.append(x_all[n_prompt:])

        proj = _matmul(x_all, w_in_p, l, tm, 768, "in_proj")
        proj_shift = _matmul(state_shift[l], w_in_p, l, nb, 768, "in_proj_shift")

        mix, s_ret = _ret_prompt(proj, n_all, ret_gn[l:l + 1], cos_p, sin_p, ret_tab, batch, seq)
        mix, s_hg = _hg_prompt(proj, mix, hg_lb, hg_gn[l:l + 1], l, batch, seq)
        mix, s_rw = _rw_prompt(proj, mix, mu, rwp, batch, seq)
        p_ret.append(_unpermute_ret_state(s_ret))
        p_hg.append(jnp.swapaxes(s_hg, -1, -2))
        p_rw.append(_unpair_rw_state(s_rw))

        (r_s, w_s, kr_s, v_s, kap_s, at_s, g_s, bonus_s, q_s, k_s, hq_s, hk_s, heg_s) = _sample_prep(
            proj, proj_shift, mu, rwp, cos_s, sin_s, hg_lb, l, n_prompt, nb)
        ret_col = lambda a: a.reshape(nb, 2, RET_HEADS, RET_DK // 2).swapaxes(1, 2).reshape(nb, RET_HEADS, RET_DK, 1)
        hg_col = lambda a: a.reshape(nb, HG_HEADS, HG_DK, 1)
        rw_row = lambda a: a.reshape(nb, RW_HEADS, 1, RW_HD)
        proj_s = proj[n_prompt:]
        cols = [ret_col(q_s), ret_col(k_s), proj_s[:, C_RV:C_RV + RET_W].reshape(nb, RET_HEADS, 1, RET_DV),
                hg_col(hq_s), hg_col(hk_s), hg_col(heg_s), proj_s[:, C_HI:C_HI + HG_W].reshape(nb, HG_HEADS, 1, HG_DV),
                rw_row(r_s), rw_row(w_s), rw_row(kr_s), v_s.reshape(nb, RW_HEADS, RW_HD, 1), rw_row(kap_s), rw_row(at_s)]
        outs = _sample_state(state_ret, state_hgrn, state_rwkv, new_sample_states, l, gam, cols)
        new_sample_states = outs[:3]
        mix = _sample_post(outs[3].reshape(nb, RET_W), outs[4].reshape(nb, HG_W), outs[5].reshape(nb, RW_W),
                           bonus_s, g_s, proj, mix, ret_gn[l:l + 1], hg_gn[l:l + 1], rwp, n_prompt, nb)

        mixed = _matmul(mix, w_out, l, tm, 1024, "out_proj")
        x1 = _add_layer_norm(x_all, mixed, ln1_g[l], ln1_b[l], tm)

        mk = _matmul(mem_flat, xa_wk, l, _pick_tile(batch * n_mem, 512, 8), XA_W, "mem_k_proj")
        mv = _matmul(mem_flat, xa_wv, l, _pick_tile(batch * n_mem, 512, 8), XA_W, "mem_v_proj")
        p_mk.append(mk.reshape(batch, n_mem, XA_HEADS, XA_HD))
        p_mv.append(mv.reshape(batch, n_mem, XA_HEADS, XA_HD))
        q_all = _matmul(x1, xa_wq, l, tm, XA_W, "xa_q_proj")
        att = _xattn_prompt(q_all, mk.reshape(batch, n_mem, XA_W), mv.reshape(batch, n_mem, XA_W), batch, seq,
                            _pick_tile(seq, 512, 8))
        att = _xattn_sample(q_all, att, cache_k, cache_v, l, n_prompt, nb)
        x2 = _add_layer_norm(x1, _matmul(att, xa_wo, l, tm, 1024, "xa_o_proj"), ln2_g[l], ln2_b[l], tm)

        comb = _router(x2, router_w, rb3, l, tm)
        moe = _moe_dense(x2.astype(BF16), comb, exp_wgu, bgu, exp_wd, bd, l, tm_moe, 256)
        x_all = _add_layer_norm(x2, moe, ln3_g[l], ln3_b[l], tm)

    y_prompt = x_all[:n_prompt].reshape(batch, seq, d)
    y_sample = x_all[n_prompt:].reshape(nb, 1, d)
    return (y_prompt, y_sample, jnp.stack(p_ret), jnp.stack(p_hg), jnp.stack(p_rw), jnp.stack(p_sh),
            jnp.stack(p_mk), jnp.stack(p_mv), new_sample_states[0], new_sample_states[1], new_sample_states[2],
            jnp.stack(s_sh))
```
